```python
import jax, jax.numpy as jnp
from jax import lax
import numpy as np

D_MODEL = 1024
BATCH = 32
SEQ = 2048
DEPTH = 4

CHUNK = 64
N_MIXERS = 4
POOL_WINDOWS = (2, 4, 8, 16)
GMLP_CHUNK = 128
GMLP_WIDTH = 2 * D_MODEL
GMLP_HEADS = 8
CONV_WIDTH = 31
SHORT_CONV_WIDTH = 3
D_FF = 2816
FFN_RES_WEIGHT = 0.5
DEEPNORM_ALPHA = (2 * DEPTH) ** 0.25
DEEPNORM_BETA = (8 * DEPTH) ** -0.25
LN_EPS = 1e-5

kernel_name = "hybrid_interleaved_pool_gmlp_conv_shortconv_deepnorm"


def _layer_norm(x, g, b):
    xf = x.astype(jnp.float32)
    mu = jnp.mean(xf, axis=-1, keepdims=True)
    var = jnp.mean(jnp.square(xf - mu), axis=-1, keepdims=True)
    return ((xf - mu) * lax.rsqrt(var + LN_EPS) * g.astype(jnp.float32)
            + b.astype(jnp.float32)).astype(x.dtype)


def _causal_depthwise_conv(x, w):
    k, c = w.shape
    return lax.conv_general_dilated(
        x, w[:, None, :].astype(x.dtype), window_strides=(1,),
        padding=[(k - 1, 0)], dimension_numbers=('NWC', 'WIO', 'NWC'),
        feature_group_count=c)


def _swiglu(x, w_in, w_out):
    gate, up = jnp.split(x @ w_in, 2, axis=-1)
    return (jax.nn.silu(gate) * up) @ w_out


def _pool_mixer(x, w_grp, scale):
    b, s, d = x.shape
    n_g = len(POOL_WINDOWS)
    xg = x.reshape(b, s, n_g, d // n_g)
    t = jnp.arange(s)
    outs = []
    for gi, win in enumerate(POOL_WINDOWS):
        xi = xg[:, :, gi, :].astype(jnp.float32)
        cs = jnp.cumsum(xi, axis=1)
        lagged = jnp.pad(cs, ((0, 0), (win, 0), (0, 0)))[:, :s]
        cnt = jnp.minimum(t + 1, win).astype(jnp.float32)[None, :, None]
        outs.append((cs - lagged) / cnt - xi)
    p = jnp.stack(outs, axis=2).astype(x.dtype)
    y = jnp.einsum('bsgc,gcd->bsgd', p, w_grp).reshape(b, s, d)
    return y * scale


def _gmlp_mixer(x, w_in, v_ln_g, v_ln_b, ws, bs, w_out):
    b, s, _ = x.shape
    z = jax.nn.gelu(x @ w_in)
    u, v = jnp.split(z, 2, axis=-1)
    v = _layer_norm(v, v_ln_g, v_ln_b)
    n_h, l, _ = ws.shape
    e = v.shape[-1]
    mask = jnp.tril(jnp.ones((l, l), dtype=bool))
    w_masked = jnp.where(mask[None], ws, jnp.zeros_like(ws)).astype(v.dtype)
    vc = v.reshape(b, s // l, l, n_h, e // n_h)
    sv = jnp.einsum('hts,bcshd->bcthd', w_masked, vc) + bs.T[None, None, :, :, None]
    return (u * sv.reshape(b, s, e)) @ w_out


def _conformer_conv(x, w_in, b_in, dw, dw_b, ln_g, ln_b, w_out, b_out):
    a, g = jnp.split(x @ w_in + b_in, 2, axis=-1)
    h = a * jax.nn.sigmoid(g)
    h = _causal_depthwise_conv(h, dw) + dw_b
    h = jax.nn.silu(_layer_norm(h, ln_g, ln_b))
    return h @ w_out + b_out


def _short_conv(x, w_in, k, w_out):
    b_gate, c_gate, h = jnp.split(x @ w_in, 3, axis=-1)
    return (b_gate * _causal_depthwise_conv(c_gate * h, k)) @ w_out


def setup_inputs(seed: int = 0) -> dict:
    key = jax.random.key(seed)
    keys = iter(jax.random.split(key, 32))
    d = D_MODEL
    n_per = [len(range(m, DEPTH, N_MIXERS)) for m in range(N_MIXERS)]
    n_a, n_b, n_c, n_d = n_per
    cg = d // len(POOL_WINDOWS)

    def nrm(shape, scale):
        return jax.random.normal(next(keys), shape, jnp.float32) * scale

    return {
        "x": nrm((BATCH, SEQ, d), 1.0),
        "ln_g": 1.0 + nrm((DEPTH, 3, d), 0.02),
        "ln_b": nrm((DEPTH, 3, d), 0.02),
        "ffn_w_in": nrm((DEPTH, 2, d, 2 * D_FF), d ** -0.5),
        "ffn_w_out": nrm((DEPTH, 2, D_FF, d), D_FF ** -0.5 * DEEPNORM_BETA),
        "pool_w": nrm((n_a, len(POOL_WINDOWS), cg, cg), cg ** -0.5 * DEEPNORM_BETA),
        "pool_scale": 1.0 + nrm((n_a, d), 0.1),
        "gmlp_w_in": nrm((n_b, d, 2 * GMLP_WIDTH), d ** -0.5),
        "gmlp_v_ln_g": 1.0 + nrm((n_b, GMLP_WIDTH), 0.02),
        "gmlp_v_ln_b": nrm((n_b, GMLP_WIDTH), 0.02),
        "gmlp_ws": nrm((n_b, GMLP_HEADS, GMLP_CHUNK, GMLP_CHUNK), GMLP_CHUNK ** -0.5),
        "gmlp_bs": 1.0 + nrm((n_b, GMLP_HEADS, GMLP_CHUNK), 0.1),
        "gmlp_w_out": nrm((n_b, GMLP_WIDTH, d), GMLP_WIDTH ** -0.5 * DEEPNORM_BETA),
        "conv_w_in": nrm((n_c, d, 2 * d), d ** -0.5),
        "conv_b_in": nrm((n_c, 2 * d), 0.02),
        "conv_dw": nrm((n_c, CONV_WIDTH, d), CONV_WIDTH ** -0.5),
        "conv_dw_b": nrm((n_c, d), 0.02),
        "conv_ln_g": 1.0 + nrm((n_c, d), 0.02),
        "conv_ln_b": nrm((n_c, d), 0.02),
        "conv_w_out": nrm((n_c, d, d), d ** -0.5 * DEEPNORM_BETA),
        "conv_b_out": nrm((n_c, d), 0.02),
        "sc_w_in": nrm((n_d, d, 3 * d), d ** -0.5),
        "sc_conv": nrm((n_d, SHORT_CONV_WIDTH, d), SHORT_CONV_WIDTH ** -0.5),
        "sc_w_out": nrm((n_d, d, d), d ** -0.5 * DEEPNORM_BETA),
    }


def reference(x, ln_g, ln_b, ffn_w_in, ffn_w_out, pool_w, pool_scale,
              gmlp_w_in, gmlp_v_ln_g, gmlp_v_ln_b, gmlp_ws, gmlp_bs, gmlp_w_out,
              conv_w_in, conv_b_in, conv_dw, conv_dw_b, conv_ln_g, conv_ln_b,
              conv_w_out, conv_b_out, sc_w_in, sc_conv, sc_w_out):
    alpha = DEEPNORM_ALPHA
    for i in range(DEPTH):
        m, j = i % N_MIXERS, i // N_MIXERS
        x = _layer_norm(alpha * x + FFN_RES_WEIGHT * _swiglu(x, ffn_w_in[i, 0], ffn_w_out[i, 0]),
                        ln_g[i, 0], ln_b[i, 0])
        if m == 0:
            h = _pool_mixer(x, pool_w[j], pool_scale[j])
        elif m == 1:
            h = _gmlp_mixer(x, gmlp_w_in[j], gmlp_v_ln_g[j], gmlp_v_ln_b[j],
                            gmlp_ws[j], gmlp_bs[j], gmlp_w_out[j])
        elif m == 2:
            h = _conformer_conv(x, conv_w_in[j], conv_b_in[j], conv_dw[j], conv_dw_b[j],
                                conv_ln_g[j], conv_ln_b[j], conv_w_out[j], conv_b_out[j])
        else:
            h = _short_conv(x, sc_w_in[j], sc_conv[j], sc_w_out[j])
        x = _layer_norm(alpha * x + h, ln_g[i, 1], ln_b[i, 1])
        x = _layer_norm(alpha * x + FFN_RES_WEIGHT * _swiglu(x, ffn_w_in[i, 1], ffn_w_out[i, 1]),
                        ln_g[i, 2], ln_b[i, 2])
    return x
```

```python
import functools

import jax
import jax.numpy as jnp
from jax import lax
from jax.experimental import pallas as pl
from jax.experimental.pallas import tpu as pltpu

DEPTH = 4
N_MIXERS = 4
POOL_WINDOWS = (2, 4, 8, 16)
GMLP_CHUNK = 128
CONV_WIDTH = 31
SHORT_CONV_WIDTH = 3
FFN_RES_WEIGHT = 0.5
DEEPNORM_ALPHA = (2 * DEPTH) ** 0.25
LN_EPS = 1e-5

BF16 = jnp.bfloat16
F32 = jnp.float32

SUBLANES = 8
POOL_HALO = 16
CONV_HALO = 32
SHORT_HALO = 8
VMEM_LIMIT = 56 * 1024 * 1024

FFN_TILE = 512
MIX_TILE = 512
GMLP_TILE = 256


def _layer_norm(r, g, b):
    mu = jnp.mean(r, axis=-1, keepdims=True)
    d = r - mu
    var = jnp.mean(d * d, axis=-1, keepdims=True)
    return d * lax.rsqrt(var + LN_EPS) * g + b


def _dot(a, b):
    return jnp.dot(a, b, preferred_element_type=F32)


def _const_spec(shape):
    nd = len(shape)
    return pl.BlockSpec(shape, lambda *_: (0,) * nd, pipeline_mode=pl.Buffered(1))


def _params(n_axes):
    return pltpu.CompilerParams(
        dimension_semantics=("arbitrary",) * n_axes,
        vmem_limit_bytes=VMEM_LIMIT,
    )


def _ffn_kernel(x_ref, win_ref, wout_ref, g_ref, b_ref, o_ref, act_ref, *, d_ff, chunk):
    x = x_ref[...]
    xb = x.astype(BF16)
    for lo in range(0, d_ff, chunk):
        gate = _dot(xb, win_ref[:, lo:lo + chunk])
        up = _dot(xb, win_ref[:, d_ff + lo:d_ff + lo + chunk])
        act_ref[:, lo:lo + chunk] = (jax.nn.silu(gate) * up).astype(BF16)
    y = _dot(act_ref[...], wout_ref[...])
    r = DEEPNORM_ALPHA * x + FFN_RES_WEIGHT * y
    o_ref[...] = _layer_norm(r, g_ref[...], b_ref[...])


def _ffn(x2, w_in, w_out, g, b):
    m, d = x2.shape
    d_ff = w_out.shape[0]
    tm = FFN_TILE
    chunk = 256
    assert m % tm == 0 and d_ff % chunk == 0
    return pl.pallas_call(
        functools.partial(_ffn_kernel, d_ff=d_ff, chunk=chunk),
        grid=(m // tm,),
        in_specs=[
            pl.BlockSpec((tm, d), lambda i: (i, 0)),
            _const_spec(w_in.shape),
            _const_spec(w_out.shape),
            _const_spec((1, d)),
            _const_spec((1, d)),
        ],
        out_specs=pl.BlockSpec((tm, d), lambda i: (i, 0)),
        out_shape=jax.ShapeDtypeStruct((m, d), F32),
        scratch_shapes=[pltpu.VMEM((tm, d_ff), BF16)],
        compiler_params=_params(1),
        name="ffn",
    )(x2, w_in, w_out, g.reshape(1, d), b.reshape(1, d))


def _pool_kernel(x_ref, halo_ref, w_ref, scale_ref, g_ref, b_ref, o_ref, *, ts, cg):
    j = pl.program_id(1)
    x = x_ref[0]
    halo = jnp.where(j > 0, halo_ref[0], 0.0)
    t = j * ts + lax.broadcasted_iota(jnp.int32, (ts, 1), 0)
    ys = []
    for gi, win in enumerate(POOL_WINDOWS):
        cols = slice(gi * cg, (gi + 1) * cg)
        xe = jnp.concatenate([halo[:, cols], x[:, cols]], axis=0)
        s = xe
        span = 1
        while span < win:
            s = s + pltpu.roll(s, span, axis=0)
            span *= 2
        s = s[POOL_HALO:]
        inv_cnt = 1.0 / jnp.minimum(t + 1, win).astype(F32)
        p = s * inv_cnt - x[:, cols]
        ys.append(_dot(p.astype(BF16), w_ref[gi]))
    y = jnp.concatenate(ys, axis=-1) * scale_ref[...]
    r = DEEPNORM_ALPHA * x + y
    o_ref[0] = _layer_norm(r, g_ref[...], b_ref[...])


def _pool_mixer(x, w_grp, scale, g, b):
    bsz, s, d = x.shape
    n_g, cg, _ = w_grp.shape
    ts = MIX_TILE
    assert s % ts == 0 and ts % POOL_HALO == 0 and max(POOL_WINDOWS) <= POOL_HALO
    hb = ts // POOL_HALO
    return pl.pallas_call(
        functools.partial(_pool_kernel, ts=ts, cg=cg),
        grid=(bsz, s // ts),
        in_specs=[
            pl.BlockSpec((1, ts, d), lambda bi, j: (bi, j, 0)),
            pl.BlockSpec((1, POOL_HALO, d), lambda bi, j: (bi, jnp.maximum(j * hb - 1, 0), 0)),
            _const_spec(w_grp.shape),
            _const_spec((1, d)),
            _const_spec((1, d)),
            _const_spec((1, d)),
        ],
        out_specs=pl.BlockSpec((1, ts, d), lambda bi, j: (bi, j, 0)),
        out_shape=jax.ShapeDtypeStruct(x.shape, F32),
        compiler_params=_params(2),
        name="pool_mixer",
    )(x, x, w_grp, scale.reshape(1, d), g.reshape(1, d), b.reshape(1, d))


def _gmlp_kernel(x_ref, win_ref, vg_ref, vb_ref, ws_ref, bst_ref, wout_ref, g_ref, b_ref,
                 o_ref, gated_ref, *, tm, e, n_h):
    x = x_ref[...]
    xb = x.astype(BF16)
    u = jax.nn.gelu(_dot(xb, win_ref[:, :e]))
    v = jax.nn.gelu(_dot(xb, win_ref[:, e:]))
    v = _layer_norm(v, vg_ref[...], vb_ref[...]).astype(BF16)
    l = GMLP_CHUNK
    dh = e // n_h
    row = lax.broadcasted_iota(jnp.int32, (l, l), 0)
    col = lax.broadcasted_iota(jnp.int32, (l, l), 1)
    causal = col <= row
    for h in range(n_h):
        w_h = jnp.where(causal, ws_ref[h], 0.0).astype(BF16)
        bias = bst_ref[:, h:h + 1]
        for c in range(tm // l):
            rows = slice(c * l, (c + 1) * l)
            cols = slice(h * dh, (h + 1) * dh)
            sv = _dot(w_h, v[rows, cols]) + bias
            gated_ref[rows, cols] = (u[rows, cols] * sv).astype(BF16)
    y = _dot(gated_ref[...], wout_ref[...])
    r = DEEPNORM_ALPHA * x + y
    o_ref[...] = _layer_norm(r, g_ref[...], b_ref[...])


def _gmlp_mixer(x2, w_in, v_g, v_b, ws, bs, w_out, g, b):
    m, d = x2.shape
    e = w_out.shape[0]
    n_h, l, _ = ws.shape
    tm = GMLP_TILE
    assert m % tm == 0 and tm % l == 0 and l == GMLP_CHUNK
    return pl.pallas_call(
        functools.partial(_gmlp_kernel, tm=tm, e=e, n_h=n_h),
        grid=(m // tm,),
        in_specs=[
            pl.BlockSpec((tm, d), lambda i: (i, 0)),
            _const_spec(w_in.shape),
            _const_spec((1, e)),
            _const_spec((1, e)),
            _const_spec(ws.shape),
            _const_spec((l, n_h)),
            _const_spec(w_out.shape),
            _const_spec((1, d)),
            _const_spec((1, d)),
        ],
        out_specs=pl.BlockSpec((tm, d), lambda i: (i, 0)),
        out_shape=jax.ShapeDtypeStruct((m, d), F32),
        scratch_shapes=[pltpu.VMEM((tm, e), BF16)],
        compiler_params=_params(1),
        name="gmlp_mixer",
    )(x2, w_in, v_g.reshape(1, e), v_b.reshape(1, e), ws, bs.T, w_out,
      g.reshape(1, d), b.reshape(1, d))


def _conv_kernel(x_ref, win_ref, bin_ref, dw_ref, dwb_ref, cg_ref, cb_ref, wout_ref, bout_ref,
                 g_ref, b_ref, o_ref, he_ref, *, ts, d):
    j = pl.program_id(1)
    x = x_ref[0]
    xb = x.astype(BF16)
    a = _dot(xb, win_ref[:, :d]) + bin_ref[:, :d]
    gt = _dot(xb, win_ref[:, d:]) + bin_ref[:, d:]
    h = a * jax.nn.sigmoid(gt)

    @pl.when(j == 0)
    def _():
        he_ref[:CONV_HALO, :] = jnp.zeros((CONV_HALO, d), F32)

    he_ref[CONV_HALO:, :] = h
    base = CONV_HALO - (CONV_WIDTH - 1)
    acc = jnp.zeros((ts, d), F32) + dwb_ref[...]
    for k in range(CONV_WIDTH):
        acc = acc + he_ref[base + k:base + k + ts, :] * dw_ref[k:k + 1, :]
    he_ref[:CONV_HALO, :] = h[ts - CONV_HALO:, :]

    hn = jax.nn.silu(_layer_norm(acc, cg_ref[...], cb_ref[...]))
    y = _dot(hn.astype(BF16), wout_ref[...]) + bout_ref[...]
    r = DEEPNORM_ALPHA * x + y
    o_ref[0] = _layer_norm(r, g_ref[...], b_ref[...])


def _conformer_conv(x, w_in, b_in, dw, dw_b, c_g, c_b, w_out, b_out, g, b):
    bsz, s, d = x.shape
    ts = MIX_TILE
    assert s % ts == 0 and CONV_WIDTH - 1 <= CONV_HALO <= ts
    return pl.pallas_call(
        functools.partial(_conv_kernel, ts=ts, d=d),
        grid=(bsz, s // ts),
        in_specs=[
            pl.BlockSpec((1, ts, d), lambda bi, j: (bi, j, 0)),
            _const_spec(w_in.shape),
            _const_spec((1, 2 * d)),
            _const_spec(dw.shape),
            _const_spec((1, d)),
            _const_spec((1, d)),
            _const_spec((1, d)),
            _const_spec(w_out.shape),
            _const_spec((1, d)),
            _const_spec((1, d)),
            _const_spec((1, d)),
        ],
        out_specs=pl.BlockSpec((1, ts, d), lambda bi, j: (bi, j, 0)),
        out_shape=jax.ShapeDtypeStruct(x.shape, F32),
        scratch_shapes=[pltpu.VMEM((CONV_HALO + ts, d), F32)],
        compiler_params=_params(2),
        name="conformer_conv",
    )(x, w_in, b_in.reshape(1, 2 * d), dw, dw_b.reshape(1, d), c_g.reshape(1, d),
      c_b.reshape(1, d), w_out, b_out.reshape(1, d), g.reshape(1, d), b.reshape(1, d))


def _short_kernel(x_ref, win_ref, k_ref, wout_ref, g_ref, b_ref, o_ref, ce_ref, *, ts, d):
    j = pl.program_id(1)
    x = x_ref[0]
    xb = x.astype(BF16)
    b_gate = _dot(xb, win_ref[:, :d])
    c_gate = _dot(xb, win_ref[:, d:2 * d])
    hh = _dot(xb, win_ref[:, 2 * d:])
    ch = c_gate * hh

    @pl.when(j == 0)
    def _():
        ce_ref[:SHORT_HALO, :] = jnp.zeros((SHORT_HALO, d), F32)

    ce_ref[SHORT_HALO:, :] = ch
    base = SHORT_HALO - (SHORT_CONV_WIDTH - 1)
    acc = ce_ref[base:base + ts, :] * k_ref[0:1, :]
    for k in range(1, SHORT_CONV_WIDTH):
        acc = acc + ce_ref[base + k:base + k + ts, :] * k_ref[k:k + 1, :]
    ce_ref[:SHORT_HALO, :] = ch[ts - SHORT_HALO:, :]

    y = _dot((b_gate * acc).astype(BF16), wout_ref[...])
    r = DEEPNORM_ALPHA * x + y
    o_ref[0] = _layer_norm(r, g_ref[...], b_ref[...])


def _short_conv(x, w_in, k, w_out, g, b):
    bsz, s, d = x.shape
    ts = MIX_TILE
    assert s % ts == 0 and SHORT_CONV_WIDTH - 1 <= SHORT_HALO <= ts
    return pl.pallas_call(
        functools.partial(_short_kernel, ts=ts, d=d),
        grid=(bsz, s // ts),
        in_specs=[
            pl.BlockSpec((1, ts, d), lambda bi, j: (bi, j, 0)),
            _const_spec(w_in.shape),
            _const_spec(k.shape),
            _const_spec(w_out.shape),
            _const_spec((1, d)),
            _const_spec((1, d)),
        ],
        out_specs=pl.BlockSpec((1, ts, d), lambda bi, j: (bi, j, 0)),
        out_shape=jax.ShapeDtypeStruct(x.shape, F32),
        scratch_shapes=[pltpu.VMEM((SHORT_HALO + ts, d), F32)],
        compiler_params=_params(2),
        name="short_conv",
    )(x, w_in, k, w_out, g.reshape(1, d), b.reshape(1, d))


def kernel(x, ln_g, ln_b, ffn_w_in, ffn_w_out, pool_w, pool_scale, gmlp_w_in, gmlp_v_ln_g, gmlp_v_ln_b, gmlp_ws, gmlp_bs, gmlp_w_out, conv_w_in, conv_b_in, conv_dw, conv_dw_b, conv_ln_g, conv_ln_b, conv_w_out, conv_b_out, sc_w_in, sc_conv, sc_w_out):
    bsz, s, d = x.shape
    m = bsz * s
    ffn_w_in = ffn_w_in.astype(BF16)
    ffn_w_out = ffn_w_out.astype(BF16)
    for i in range(DEPTH):
        mi, j = i % N_MIXERS, i // N_MIXERS
        x = _ffn(x.reshape(m, d), ffn_w_in[i, 0], ffn_w_out[i, 0], ln_g[i, 0], ln_b[i, 0])
        if mi == 0:
            x = _pool_mixer(x.reshape(bsz, s, d), pool_w[j].astype(BF16), pool_scale[j],
                            ln_g[i, 1], ln_b[i, 1])
        elif mi == 1:
            x = _gmlp_mixer(x, gmlp_w_in[j].astype(BF16), gmlp_v_ln_g[j], gmlp_v_ln_b[j],
                            gmlp_ws[j], gmlp_bs[j], gmlp_w_out[j].astype(BF16),
                            ln_g[i, 1], ln_b[i, 1])
        elif mi == 2:
            x = _conformer_conv(x.reshape(bsz, s, d), conv_w_in[j].astype(BF16), conv_b_in[j],
                                conv_dw[j], conv_dw_b[j], conv_ln_g[j], conv_ln_b[j],
                                conv_w_out[j].astype(BF16), conv_b_out[j], ln_g[i, 1], ln_b[i, 1])
        else:
            x = _short_conv(x.reshape(bsz, s, d), sc_w_in[j].astype(BF16), sc_conv[j],
                            sc_w_out[j].astype(BF16), ln_g[i, 1], ln_b[i, 1])
        x = _ffn(x.reshape(m, d), ffn_w_in[i, 1], ffn_w_out[i, 1], ln_g[i, 2], ln_b[i, 2])
    return x.reshape(bsz, s, d)
```

```python
import functools

import jax
import jax.numpy as jnp
from jax import lax
from jax.experimental import pallas as pl
from jax.experimental.pallas import tpu as pltpu

DEPTH = 4
N_MIXERS = 4
POOL_WINDOWS = (2, 4, 8, 16)
GMLP_CHUNK = 128
CONV_WIDTH = 31
SHORT_CONV_WIDTH = 3
FFN_RES_WEIGHT = 0.5
DEEPNORM_ALPHA = (2 * DEPTH) ** 0.25
LN_EPS = 1e-5

BF16 = jnp.bfloat16
F32 = jnp.float32

SUBLANES = 8
LANES = 128
MXU_COLS = 256
CONV_FRAMES = 8
POOL_HALO = 16
CONV_HALO = 32
SHORT_HALO = 8
VMEM_LIMIT = 56 * 1024 * 1024

FFN_TILE = 512
MIX_TILE = 512
GMLP_TILE = 256


def _layer_norm(r, g, b):
    mu = jnp.mean(r, axis=-1, keepdims=True)
    d = r - mu
    var = jnp.mean(d * d, axis=-1, keepdims=True)
    return d * lax.rsqrt(var + LN_EPS) * g + b


def _zero_after(v):
    u = lax.bitcast_convert_type(v, jnp.uint32)
    u = lax.shift_right_logical(lax.shift_right_logical(u, jnp.uint32(16)), jnp.uint32(16))
    return lax.bitcast_convert_type(u, F32)


def _dot(a, b):
    return jnp.dot(a, b, preferred_element_type=F32)


def _const_spec(shape):
    nd = len(shape)
    return pl.BlockSpec(shape, lambda *_: (0,) * nd, pipeline_mode=pl.Buffered(1))


def _params(n_axes):
    return pltpu.CompilerParams(
        dimension_semantics=("arbitrary",) * n_axes,
        vmem_limit_bytes=VMEM_LIMIT,
    )


def _ffn_kernel(x_ref, win_ref, wout_ref, g_ref, b_ref, o_ref, act_ref, r_ref, *, n, d_ff, chunk):
    i = pl.program_id(0)

    @pl.when(i == 0)
    def _():
        r_ref[...] = jnp.zeros_like(r_ref)

    @pl.when(i < n)
    def _():
        prev = _layer_norm(r_ref[...], g_ref[...], b_ref[...])
        o_ref[...] = prev
        anchor = _zero_after(prev)
        x = x_ref[...]
        xb = x.astype(BF16)
        for lo in range(0, d_ff, chunk):
            gate = _dot(xb, win_ref[:, lo:lo + chunk])
            up = _dot(xb, win_ref[:, d_ff + lo:d_ff + lo + chunk])
            if lo + chunk <= anchor.shape[1]:
                up = up + anchor[:, lo:lo + chunk]
            act_ref[:, lo:lo + chunk] = (jax.nn.silu(gate) * up).astype(BF16)
        y = _dot(act_ref[...], wout_ref[...])
        r_ref[...] = DEEPNORM_ALPHA * x + FFN_RES_WEIGHT * y

    @pl.when(i == n)
    def _():
        o_ref[...] = _layer_norm(r_ref[...], g_ref[...], b_ref[...])


def _ffn(x2, w_in, w_out, g, b):
    m, d = x2.shape
    d_ff = w_out.shape[0]
    tm = FFN_TILE
    chunk = 256
    assert m % tm == 0 and d_ff % chunk == 0
    n = m // tm
    return pl.pallas_call(
        functools.partial(_ffn_kernel, n=n, d_ff=d_ff, chunk=chunk),
        grid=(n + 1,),
        in_specs=[
            pl.BlockSpec((tm, d), lambda i: (jnp.minimum(i, n - 1), 0)),
            _const_spec(w_in.shape),
            _const_spec(w_out.shape),
            _const_spec((1, d)),
            _const_spec((1, d)),
        ],
        out_specs=pl.BlockSpec((tm, d), lambda i: (jnp.maximum(i - 1, 0), 0)),
        out_shape=jax.ShapeDtypeStruct((m, d), F32),
        scratch_shapes=[pltpu.VMEM((tm, d_ff), BF16), pltpu.VMEM((tm, d), F32)],
        compiler_params=_params(1),
        name="ffn",
    )(x2, w_in, w_out, g.reshape(1, d), b.reshape(1, d))


def _pool_kernel(x_ref, halo_ref, w_ref, scale_ref, g_ref, b_ref, o_ref, *, ts, cg):
    j = pl.program_id(1)
    x = x_ref[0]
    halo = jnp.where(j > 0, halo_ref[0], 0.0)
    t = j * ts + lax.broadcasted_iota(jnp.int32, (ts, 1), 0)
    ys = []
    for gi, win in enumerate(POOL_WINDOWS):
        cols = slice(gi * cg, (gi + 1) * cg)
        xe = jnp.concatenate([halo[:, cols], x[:, cols]], axis=0)
        s = xe
        span = 1
        while span < win:
            s = s + pltpu.roll(s, span, axis=0)
            span *= 2
        s = s[POOL_HALO:]
        inv_cnt = 1.0 / jnp.minimum(t + 1, win).astype(F32)
        p = s * inv_cnt - x[:, cols]
        ys.append(_dot(p.astype(BF16), w_ref[gi]))
    y = jnp.concatenate(ys, axis=-1) * scale_ref[...]
    r = DEEPNORM_ALPHA * x + y
    o_ref[0] = _layer_norm(r, g_ref[...], b_ref[...])


def _pool_mixer(x, w_grp, scale, g, b):
    bsz, s, d = x.shape
    n_g, cg, _ = w_grp.shape
    ts = MIX_TILE
    assert s % ts == 0 and ts % POOL_HALO == 0 and max(POOL_WINDOWS) <= POOL_HALO
    hb = ts // POOL_HALO
    return pl.pallas_call(
        functools.partial(_pool_kernel, ts=ts, cg=cg),
        grid=(bsz, s // ts),
        in_specs=[
            pl.BlockSpec((1, ts, d), lambda bi, j: (bi, j, 0)),
            pl.BlockSpec((1, POOL_HALO, d), lambda bi, j: (bi, jnp.maximum(j * hb - 1, 0), 0)),
            _const_spec(w_grp.shape),
            _const_spec((1, d)),
            _const_spec((1, d)),
            _const_spec((1, d)),
        ],
        out_specs=pl.BlockSpec((1, ts, d), lambda bi, j: (bi, j, 0)),
        out_shape=jax.ShapeDtypeStruct(x.shape, F32),
        compiler_params=_params(2),
        name="pool_mixer",
    )(x, x, w_grp, scale.reshape(1, d), g.reshape(1, d), b.reshape(1, d))


def _gmlp_kernel(x_ref, win_ref, vg_ref, vb_ref, ws_ref, bst_ref, wout_ref, g_ref, b_ref,
                 o_ref, gated_ref, *, tm, e, n_h):
    x = x_ref[...]
    xb = x.astype(BF16)
    u = jax.nn.gelu(_dot(xb, win_ref[:, :e]))
    v = jax.nn.gelu(_dot(xb, win_ref[:, e:]))
    v = _layer_norm(v, vg_ref[...], vb_ref[...]).astype(BF16)
    l = GMLP_CHUNK
    dh = e // n_h
    row = lax.broadcasted_iota(jnp.int32, (l, l), 0)
    col = lax.broadcasted_iota(jnp.int32, (l, l), 1)
    causal = col <= row
    for h in range(n_h):
        w_h = jnp.where(causal, ws_ref[h], 0.0).astype(BF16)
        bias = bst_ref[:, h:h + 1]
        for c in range(tm // l):
            rows = slice(c * l, (c + 1) * l)
            cols = slice(h * dh, (h + 1) * dh)
            sv = _dot(w_h, v[rows, cols]) + bias
            gated_ref[rows, cols] = (u[rows, cols] * sv).astype(BF16)
    y = _dot(gated_ref[...], wout_ref[...])
    r = DEEPNORM_ALPHA * x + y
    o_ref[...] = _layer_norm(r, g_ref[...], b_ref[...])


def _gmlp_mixer(x2, w_in, v_g, v_b, ws, bs, w_out, g, b):
    m, d = x2.shape
    e = w_out.shape[0]
    n_h, l, _ = ws.shape
    tm = GMLP_TILE
    assert m % tm == 0 and tm % l == 0 and l == GMLP_CHUNK
    return pl.pallas_call(
        functools.partial(_gmlp_kernel, tm=tm, e=e, n_h=n_h),
        grid=(m // tm,),
        in_specs=[
            pl.BlockSpec((tm, d), lambda i: (i, 0)),
            _const_spec(w_in.shape),
            _const_spec((1, e)),
            _const_spec((1, e)),
            _const_spec(ws.shape),
            _const_spec((l, n_h)),
            _const_spec(w_out.shape),
            _const_spec((1, d)),
            _const_spec((1, d)),
        ],
        out_specs=pl.BlockSpec((tm, d), lambda i: (i, 0)),
        out_shape=jax.ShapeDtypeStruct((m, d), F32),
        scratch_shapes=[pltpu.VMEM((tm, e), BF16)],
        compiler_params=_params(1),
        name="gmlp_mixer",
    )(x2, w_in, v_g.reshape(1, e), v_b.reshape(1, e), ws, bs.T, w_out,
      g.reshape(1, d), b.reshape(1, d))


def _conv_kernel(x_ref, win_ref, bin_ref, dw_ref, dwb_ref, cg_ref, cb_ref, wout_ref, bout_ref,
                 g_ref, b_ref, o_ref, z_ref, cz_ref, *, ts, d):
    j = pl.program_id(1)
    nl = d // LANES
    x = x_ref[0]
    xb = x.astype(BF16)

    @pl.when(j == 0)
    def _():
        z_ref[:CONV_HALO * nl, :] = jnp.zeros((CONV_HALO * nl, LANES), F32)

    for c in range(0, d, MXU_COLS):
        a = _dot(xb, win_ref[:, c:c + MXU_COLS]) + bin_ref[:, c:c + MXU_COLS]
        gt = _dot(xb, win_ref[:, d + c:d + c + MXU_COLS]) + bin_ref[:, d + c:d + c + MXU_COLS]
        h = a * jax.nn.sigmoid(gt)
        for q in range(MXU_COLS // LANES):
            lt = c // LANES + q
            z_ref[pl.ds(CONV_HALO * nl + lt, ts, stride=nl), :] = h[:, q * LANES:(q + 1) * LANES]

    base = CONV_HALO - (CONV_WIDTH - 1)
    bias = dwb_ref[...]
    for t0 in range(0, ts, CONV_FRAMES):
        frames = {}
        accs = [bias] * CONV_FRAMES
        for k in range(CONV_WIDTH):
            w = dw_ref[k * nl:(k + 1) * nl, :]
            for i in range(CONV_FRAMES):
                e = t0 + base + i + k
                if e not in frames:
                    frames[e] = z_ref[e * nl:(e + 1) * nl, :]
                accs[i] = accs[i] + frames[e] * w
        for i in range(CONV_FRAMES):
            cz_ref[(t0 + i) * nl:(t0 + i + 1) * nl, :] = accs[i]
    z_ref[:CONV_HALO * nl, :] = z_ref[ts * nl:(ts + CONV_HALO) * nl, :]

    acc = jnp.concatenate([cz_ref[pl.ds(lt, ts, stride=nl), :] for lt in range(nl)], axis=-1)
    hn = jax.nn.silu(_layer_norm(acc, cg_ref[...], cb_ref[...]))
    y = _dot(hn.astype(BF16), wout_ref[...]) + bout_ref[...]
    r = DEEPNORM_ALPHA * x + y
    o_ref[0] = _layer_norm(r, g_ref[...], b_ref[...])


def _conformer_conv(x, w_in, b_in, dw, dw_b, c_g, c_b, w_out, b_out, g, b):
    bsz, s, d = x.shape
    ts = MIX_TILE
    assert s % ts == 0 and CONV_WIDTH - 1 <= CONV_HALO <= ts
    assert d % MXU_COLS == 0 and ts % CONV_FRAMES == 0
    nl = d // LANES
    n_taps = dw.shape[0]
    return pl.pallas_call(
        functools.partial(_conv_kernel, ts=ts, d=d),
        grid=(bsz, s // ts),
        in_specs=[
            pl.BlockSpec((1, ts, d), lambda bi, j: (bi, j, 0)),
            _const_spec(w_in.shape),
            _const_spec((1, 2 * d)),
            _const_spec((n_taps * nl, LANES)),
            _const_spec((nl, LANES)),
            _const_spec((1, d)),
            _const_spec((1, d)),
            _const_spec(w_out.shape),
            _const_spec((1, d)),
            _const_spec((1, d)),
            _const_spec((1, d)),
        ],
        out_specs=pl.BlockSpec((1, ts, d), lambda bi, j: (bi, j, 0)),
        out_shape=jax.ShapeDtypeStruct(x.shape, F32),
        scratch_shapes=[pltpu.VMEM(((CONV_HALO + ts) * nl, LANES), F32),
                        pltpu.VMEM((ts * nl, LANES), F32)],
        compiler_params=_params(2),
        name="conformer_conv",
    )(x, w_in, b_in.reshape(1, 2 * d), dw.reshape(n_taps * nl, LANES), dw_b.reshape(nl, LANES),
      c_g.reshape(1, d), c_b.reshape(1, d), w_out, b_out.reshape(1, d), g.reshape(1, d),
      b.reshape(1, d))


def _short_kernel(x_ref, win_ref, k_ref, wout_ref, g_ref, b_ref, o_ref, ce_ref, *, ts, d):
    j = pl.program_id(1)
    x = x_ref[0]
    xb = x.astype(BF16)
    b_gate = _dot(xb, win_ref[:, :d])
    c_gate = _dot(xb, win_ref[:, d:2 * d])
    hh = _dot(xb, win_ref[:, 2 * d:])
    ch = c_gate * hh

    @pl.when(j == 0)
    def _():
        ce_ref[:SHORT_HALO, :] = jnp.zeros((SHORT_HALO, d), F32)

    ce_ref[SHORT_HALO:, :] = ch
    base = SHORT_HALO - (SHORT_CONV_WIDTH - 1)
    acc = ce_ref[base:base + ts, :] * k_ref[0:1, :]
    for k in range(1, SHORT_CONV_WIDTH):
        acc = acc + ce_ref[base + k:base + k + ts, :] * k_ref[k:k + 1, :]
    ce_ref[:SHORT_HALO, :] = ch[ts - SHORT_HALO:, :]

    y = _dot((b_gate * acc).astype(BF16), wout_ref[...])
    r = DEEPNORM_ALPHA * x + y
    o_ref[0] = _layer_norm(r, g_ref[...], b_ref[...])


def _short_conv(x, w_in, k, w_out, g, b):
    bsz, s, d = x.shape
    ts = MIX_TILE
    assert s % ts == 0 and SHORT_CONV_WIDTH - 1 <= SHORT_HALO <= ts
    return pl.pallas_call(
        functools.partial(_short_kernel, ts=ts, d=d),
        grid=(bsz, s // ts),
        in_specs=[
            pl.BlockSpec((1, ts, d), lambda bi, j: (bi, j, 0)),
            _const_spec(w_in.shape),
            _const_spec(k.shape),
            _const_spec(w_out.shape),
            _const_spec((1, d)),
            _const_spec((1, d)),
        ],
        out_specs=pl.BlockSpec((1, ts, d), lambda bi, j: (bi, j, 0)),
        out_shape=jax.ShapeDtypeStruct(x.shape, F32),
        scratch_shapes=[pltpu.VMEM((SHORT_HALO + ts, d), F32)],
        compiler_params=_params(2),
        name="short_conv",
    )(x, w_in, k, w_out, g.reshape(1, d), b.reshape(1, d))


def kernel(x, ln_g, ln_b, ffn_w_in, ffn_w_out, pool_w, pool_scale, gmlp_w_in, gmlp_v_ln_g, gmlp_v_ln_b, gmlp_ws, gmlp_bs, gmlp_w_out, conv_w_in, conv_b_in, conv_dw, conv_dw_b, conv_ln_g, conv_ln_b, conv_w_out, conv_b_out, sc_w_in, sc_conv, sc_w_out):
    bsz, s, d = x.shape
    m = bsz * s
    ffn_w_in = ffn_w_in.astype(BF16)
    ffn_w_out = ffn_w_out.astype(BF16)
    for i in range(DEPTH):
        mi, j = i % N_MIXERS, i // N_MIXERS
        x = _ffn(x.reshape(m, d), ffn_w_in[i, 0], ffn_w_out[i, 0], ln_g[i, 0], ln_b[i, 0])
        if mi == 0:
            x = _pool_mixer(x.reshape(bsz, s, d), pool_w[j].astype(BF16), pool_scale[j],
                            ln_g[i, 1], ln_b[i, 1])
        elif mi == 1:
            x = _gmlp_mixer(x, gmlp_w_in[j].astype(BF16), gmlp_v_ln_g[j], gmlp_v_ln_b[j],
                            gmlp_ws[j], gmlp_bs[j], gmlp_w_out[j].astype(BF16),
                            ln_g[i, 1], ln_b[i, 1])
        elif mi == 2:
            x = _conformer_conv(x.reshape(bsz, s, d), conv_w_in[j].astype(BF16), conv_b_in[j],
                                conv_dw[j], conv_dw_b[j], conv_ln_g[j], conv_ln_b[j],
                                conv_w_out[j].astype(BF16), conv_b_out[j], ln_g[i, 1], ln_b[i, 1])
        else:
            x = _short_conv(x.reshape(bsz, s, d), sc_w_in[j].astype(BF16), sc_conv[j],
                            sc_w_out[j].astype(BF16), ln_g[i, 1], ln_b[i, 1])
        x = _ffn(x.reshape(m, d), ffn_w_in[i, 1], ffn_w_out[i, 1], ln_g[i, 2], ln_b[i, 2])
    return x.reshape(bsz, s, d)
```

```python
import functools

import jax
import jax.numpy as jnp
from jax import lax
from jax.experimental import pallas as pl
from jax.experimental.pallas import tpu as pltpu

DEPTH = 4
N_MIXERS = 4
POOL_WINDOWS = (2, 4, 8, 16)
GMLP_CHUNK = 128
CONV_WIDTH = 31
SHORT_CONV_WIDTH = 3
FFN_RES_WEIGHT = 0.5
DEEPNORM_ALPHA = (2 * DEPTH) ** 0.25
LN_EPS = 1e-5

BF16 = jnp.bfloat16
F32 = jnp.float32

SUBLANES = 8
LANES = 128
MXU_COLS = 256
PAIR_COLS = 2 * MXU_COLS
CONV_FRAMES = 8
POOL_HALO = 16
CONV_HALO = 32
SHORT_HALO = 8
VMEM_LIMIT = 56 * 1024 * 1024

FFN_TILE = 512
MIX_TILE = 512


def _layer_norm(r, g, b):
    mu = jnp.mean(r, axis=-1, keepdims=True)
    d = r - mu
    var = jnp.mean(d * d, axis=-1, keepdims=True)
    return d * lax.rsqrt(var + LN_EPS) * g + b


def _zero_after(v):
    u = lax.bitcast_convert_type(v, jnp.uint32)
    u = lax.shift_right_logical(lax.shift_right_logical(u, jnp.uint32(16)), jnp.uint32(16))
    return lax.bitcast_convert_type(u, F32)


def _dot(a, b):
    return jnp.dot(a, b, preferred_element_type=F32)


def _const_spec(shape):
    nd = len(shape)
    return pl.BlockSpec(shape, lambda *_: (0,) * nd, pipeline_mode=pl.Buffered(1))


def _params(n_axes):
    return pltpu.CompilerParams(
        dimension_semantics=("arbitrary",) * n_axes,
        vmem_limit_bytes=VMEM_LIMIT,
    )


def _delayed_norm(i, n, r_ref, g_ref, b_ref, o_ref, main):
    @pl.when(i == 0)
    def _():
        r_ref[...] = jnp.zeros_like(r_ref)

    @pl.when(i < n)
    def _():
        prev = _layer_norm(r_ref[...], g_ref[...], b_ref[...])
        o_ref[...] = prev
        main(_zero_after(prev))

    @pl.when(i == n)
    def _():
        o_ref[...] = _layer_norm(r_ref[...], g_ref[...], b_ref[...])


def _tile_in_spec(tm, d, n):
    return pl.BlockSpec((tm, d), lambda i: (jnp.minimum(i, n - 1), 0))


def _tile_out_spec(tm, d):
    return pl.BlockSpec((tm, d), lambda i: (jnp.maximum(i - 1, 0), 0))


def _ffn_kernel(x_ref, win_ref, wout_ref, g_ref, b_ref, o_ref, act_ref, r_ref, *, n, d_ff):
    def main(anchor):
        x = x_ref[...]
        xb = x.astype(BF16)
        for lo in range(0, d_ff, MXU_COLS):
            gate = _dot(xb, win_ref[:, lo:lo + MXU_COLS])
            up = _dot(xb, win_ref[:, d_ff + lo:d_ff + lo + MXU_COLS])
            if lo + MXU_COLS <= anchor.shape[1]:
                up = up + anchor[:, lo:lo + MXU_COLS]
            act_ref[:, lo:lo + MXU_COLS] = (jax.nn.silu(gate) * up).astype(BF16)
        y = _dot(act_ref[...], wout_ref[...])
        r_ref[...] = DEEPNORM_ALPHA * x + FFN_RES_WEIGHT * y

    _delayed_norm(pl.program_id(0), n, r_ref, g_ref, b_ref, o_ref, main)


def _ffn(x2, w_in, w_out, g, b):
    m, d = x2.shape
    d_ff = w_out.shape[0]
    tm = FFN_TILE
    assert m % tm == 0 and d_ff % MXU_COLS == 0
    n = m // tm
    return pl.pallas_call(
        functools.partial(_ffn_kernel, n=n, d_ff=d_ff),
        grid=(n + 1,),
        in_specs=[
            _tile_in_spec(tm, d, n),
            _const_spec(w_in.shape),
            _const_spec(w_out.shape),
            _const_spec((1, d)),
            _const_spec((1, d)),
        ],
        out_specs=_tile_out_spec(tm, d),
        out_shape=jax.ShapeDtypeStruct((m, d), F32),
        scratch_shapes=[pltpu.VMEM((tm, d_ff), BF16), pltpu.VMEM((tm, d), F32)],
        compiler_params=_params(1),
        name="ffn",
    )(x2, w_in, w_out, g.reshape(1, d), b.reshape(1, d))


def _pool_kernel(x_ref, halo_ref, w_ref, scale_ref, g_ref, b_ref, o_ref, *, ts, cg):
    j = pl.program_id(1)
    x = x_ref[0]
    halo = jnp.where(j > 0, halo_ref[0], 0.0)
    t = j * ts + lax.broadcasted_iota(jnp.int32, (ts, 1), 0)
    ys = []
    for gi, win in enumerate(POOL_WINDOWS):
        cols = slice(gi * cg, (gi + 1) * cg)
        xe = jnp.concatenate([halo[:, cols], x[:, cols]], axis=0)
        s = xe
        span = 1
        while span < win:
            s = s + pltpu.roll(s, span, axis=0)
            span *= 2
        s = s[POOL_HALO:]
        inv_cnt = 1.0 / jnp.minimum(t + 1, win).astype(F32)
        p = s * inv_cnt - x[:, cols]
        ys.append(_dot(p.astype(BF16), w_ref[gi]))
    y = jnp.concatenate(ys, axis=-1) * scale_ref[...]
    r = DEEPNORM_ALPHA * x + y
    o_ref[0] = _layer_norm(r, g_ref[...], b_ref[...])


def _pool_mixer(x, w_grp, scale, g, b):
    bsz, s, d = x.shape
    n_g, cg, _ = w_grp.shape
    ts = MIX_TILE
    assert s % ts == 0 and ts % POOL_HALO == 0 and max(POOL_WINDOWS) <= POOL_HALO
    hb = ts // POOL_HALO
    return pl.pallas_call(
        functools.partial(_pool_kernel, ts=ts, cg=cg),
        grid=(bsz, s // ts),
        in_specs=[
            pl.BlockSpec((1, ts, d), lambda bi, j: (bi, j, 0)),
            pl.BlockSpec((1, POOL_HALO, d), lambda bi, j: (bi, jnp.maximum(j * hb - 1, 0), 0)),
            _const_spec(w_grp.shape),
            _const_spec((1, d)),
            _const_spec((1, d)),
            _const_spec((1, d)),
        ],
        out_specs=pl.BlockSpec((1, ts, d), lambda bi, j: (bi, j, 0)),
        out_shape=jax.ShapeDtypeStruct(x.shape, F32),
        compiler_params=_params(2),
        name="pool_mixer",
    )(x, x, w_grp, scale.reshape(1, d), g.reshape(1, d), b.reshape(1, d))


def _gmlp_kernel(x_ref, win_ref, vg_ref, vb_ref, ws_ref, bst_ref, wout_ref, g_ref, b_ref,
                 o_ref, v_ref, gated_ref, r_ref, *, n, tm, e, n_h):
    l = GMLP_CHUNK
    dh = e // n_h

    def main(anchor):
        x = x_ref[...]
        xb = x.astype(BF16)
        row_sum = jnp.zeros((tm, 1), F32)
        for c in range(0, e, PAIR_COLS):
            vc = _dot(xb, win_ref[:, e + c:e + c + PAIR_COLS])
            if c + PAIR_COLS <= anchor.shape[1]:
                vc = vc + anchor[:, c:c + PAIR_COLS]
            vc = jax.nn.gelu(vc)
            v_ref[:, c:c + PAIR_COLS] = vc
            row_sum = row_sum + jnp.sum(vc, axis=-1, keepdims=True)
        mu = row_sum * (1.0 / e)
        sq_sum = jnp.zeros((tm, 1), F32)
        for c in range(0, e, PAIR_COLS):
            dv = v_ref[:, c:c + PAIR_COLS] - mu
            sq_sum = sq_sum + jnp.sum(dv * dv, axis=-1, keepdims=True)
        rstd = lax.rsqrt(sq_sum * (1.0 / e) + LN_EPS)

        row = lax.broadcasted_iota(jnp.int32, (l, l), 0)
        col = lax.broadcasted_iota(jnp.int32, (l, l), 1)
        causal = col <= row
        for c in range(0, e, PAIR_COLS):
            pair = slice(c, c + PAIR_COLS)
            vn = ((v_ref[:, pair] - mu) * rstd * vg_ref[:, pair] + vb_ref[:, pair]).astype(BF16)
            u = jax.nn.gelu(_dot(xb, win_ref[:, pair]))
            for q in range(0, PAIR_COLS, dh):
                h = (c + q) // dh
                w_h = jnp.where(causal, ws_ref[h], 0.0).astype(BF16)
                bias = bst_ref[:, h:h + 1]
                for rc in range(0, tm, l):
                    sv = _dot(w_h, vn[rc:rc + l, q:q + dh]) + bias
                    gated_ref[rc:rc + l, c + q:c + q + dh] = (u[rc:rc + l, q:q + dh] * sv).astype(BF16)
        y = _dot(gated_ref[...], wout_ref[...])
        r_ref[...] = DEEPNORM_ALPHA * x + y

    _delayed_norm(pl.program_id(0), n, r_ref, g_ref, b_ref, o_ref, main)


def _gmlp_mixer(x2, w_in, v_g, v_b, ws, bs, w_out, g, b):
    m, d = x2.shape
    e = w_out.shape[0]
    n_h, l, _ = ws.shape
    tm = MIX_TILE
    assert m % tm == 0 and tm % l == 0 and l == GMLP_CHUNK
    assert e % PAIR_COLS == 0 and PAIR_COLS % (e // n_h) == 0 and (e // n_h) % LANES == 0
    n = m // tm
    return pl.pallas_call(
        functools.partial(_gmlp_kernel, n=n, tm=tm, e=e, n_h=n_h),
        grid=(n + 1,),
        in_specs=[
            _tile_in_spec(tm, d, n),
            _const_spec(w_in.shape),
            _const_spec((1, e)),
            _const_spec((1, e)),
            _const_spec(ws.shape),
            _const_spec((l, n_h)),
            _const_spec(w_out.shape),
            _const_spec((1, d)),
            _const_spec((1, d)),
        ],
        out_specs=_tile_out_spec(tm, d),
        out_shape=jax.ShapeDtypeStruct((m, d), F32),
        scratch_shapes=[pltpu.VMEM((tm, e), F32), pltpu.VMEM((tm, e), BF16),
                        pltpu.VMEM((tm, d), F32)],
        compiler_params=_params(1),
        name="gmlp_mixer",
    )(x2, w_in, v_g.reshape(1, e), v_b.reshape(1, e), ws, bs.T, w_out,
      g.reshape(1, d), b.reshape(1, d))


def _conv_kernel(x_ref, win_ref, bin_ref, dw_ref, dwb_ref, cg_ref, cb_ref, wout_ref, bout_ref,
                 g_ref, b_ref, o_ref, z_ref, cz_ref, r_ref, *, n, tiles_per_seq, ts, d):
    i = pl.program_id(0)
    nl = d // LANES

    @pl.when(i % tiles_per_seq == 0)
    def _():
        z_ref[:CONV_HALO * nl, :] = jnp.zeros((CONV_HALO * nl, LANES), F32)

    def main(anchor):
        x = x_ref[...]
        xb = x.astype(BF16)
        for c in range(0, d, MXU_COLS):
            a = _dot(xb, win_ref[:, c:c + MXU_COLS]) + bin_ref[:, c:c + MXU_COLS]
            a = a + anchor[:, c:c + MXU_COLS]
            gt = _dot(xb, win_ref[:, d + c:d + c + MXU_COLS]) + bin_ref[:, d + c:d + c + MXU_COLS]
            h = a * jax.nn.sigmoid(gt)
            for q in range(MXU_COLS // LANES):
                lt = c // LANES + q
                z_ref[pl.ds(CONV_HALO * nl + lt, ts, stride=nl), :] = h[:, q * LANES:(q + 1) * LANES]

        base = CONV_HALO - (CONV_WIDTH - 1)
        bias = dwb_ref[...]
        for t0 in range(0, ts, CONV_FRAMES):
            frames = {}
            accs = [bias] * CONV_FRAMES
            for k in range(CONV_WIDTH):
                w = dw_ref[k * nl:(k + 1) * nl, :]
                for f in range(CONV_FRAMES):
                    e = t0 + base + f + k
                    if e not in frames:
                        frames[e] = z_ref[e * nl:(e + 1) * nl, :]
                    accs[f] = accs[f] + frames[e] * w
            for f in range(CONV_FRAMES):
                cz_ref[(t0 + f) * nl:(t0 + f + 1) * nl, :] = accs[f]
        z_ref[:CONV_HALO * nl, :] = z_ref[ts * nl:(ts + CONV_HALO) * nl, :]

        acc = jnp.concatenate([cz_ref[pl.ds(lt, ts, stride=nl), :] for lt in range(nl)], axis=-1)
        hn = jax.nn.silu(_layer_norm(acc, cg_ref[...], cb_ref[...]))
        y = _dot(hn.astype(BF16), wout_ref[...]) + bout_ref[...]
        r_ref[...] = DEEPNORM_ALPHA * x + y

    _delayed_norm(i, n, r_ref, g_ref, b_ref, o_ref, main)


def _conformer_conv(x2, seq_len, w_in, b_in, dw, dw_b, c_g, c_b, w_out, b_out, g, b):
    m, d = x2.shape
    ts = MIX_TILE
    assert seq_len % ts == 0 and m % seq_len == 0 and CONV_WIDTH - 1 <= CONV_HALO <= ts
    assert d % MXU_COLS == 0 and ts % CONV_FRAMES == 0
    nl = d // LANES
    n_taps = dw.shape[0]
    n = m // ts
    return pl.pallas_call(
        functools.partial(_conv_kernel, n=n, tiles_per_seq=seq_len // ts, ts=ts, d=d),
        grid=(n + 1,),
        in_specs=[
            _tile_in_spec(ts, d, n),
            _const_spec(w_in.shape),
            _const_spec((1, 2 * d)),
            _const_spec((n_taps * nl, LANES)),
            _const_spec((nl, LANES)),
            _const_spec((1, d)),
            _const_spec((1, d)),
            _const_spec(w_out.shape),
            _const_spec((1, d)),
            _const_spec((1, d)),
            _const_spec((1, d)),
        ],
        out_specs=_tile_out_spec(ts, d),
        out_shape=jax.ShapeDtypeStruct((m, d), F32),
        scratch_shapes=[pltpu.VMEM(((CONV_HALO + ts) * nl, LANES), F32),
                        pltpu.VMEM((ts * nl, LANES), F32),
                        pltpu.VMEM((ts, d), F32)],
        compiler_params=_params(1),
        name="conformer_conv",
    )(x2, w_in, b_in.reshape(1, 2 * d), dw.reshape(n_taps * nl, LANES), dw_b.reshape(nl, LANES),
      c_g.reshape(1, d), c_b.reshape(1, d), w_out, b_out.reshape(1, d), g.reshape(1, d),
      b.reshape(1, d))


def _short_kernel(x_ref, win_ref, k_ref, wout_ref, g_ref, b_ref, o_ref, ce_ref, gated_ref, r_ref,
                  *, n, tiles_per_seq, ts, d):
    i = pl.program_id(0)

    @pl.when(i % tiles_per_seq == 0)
    def _():
        ce_ref[:SHORT_HALO, :] = jnp.zeros((SHORT_HALO, d), F32)

    def main(anchor):
        x = x_ref[...]
        xb = x.astype(BF16)
        for c in range(0, d, MXU_COLS):
            c_gate = _dot(xb, win_ref[:, d + c:d + c + MXU_COLS]) + anchor[:, c:c + MXU_COLS]
            hh = _dot(xb, win_ref[:, 2 * d + c:2 * d + c + MXU_COLS])
            ce_ref[SHORT_HALO:, c:c + MXU_COLS] = c_gate * hh
        base = SHORT_HALO - (SHORT_CONV_WIDTH - 1)
        for c in range(0, d, PAIR_COLS):
            cols = slice(c, c + PAIR_COLS)
            b_gate = _dot(xb, win_ref[:, cols])
            acc = ce_ref[base:base + ts, cols] * k_ref[0:1, cols]
            for k in range(1, SHORT_CONV_WIDTH):
                acc = acc + ce_ref[base + k:base + k + ts, cols] * k_ref[k:k + 1, cols]
            gated_ref[:, cols] = (b_gate * acc).astype(BF16)
        ce_ref[:SHORT_HALO, :] = ce_ref[ts:ts + SHORT_HALO, :]

        y = _dot(gated_ref[...], wout_ref[...])
        r_ref[...] = DEEPNORM_ALPHA * x + y

    _delayed_norm(i, n, r_ref, g_ref, b_ref, o_ref, main)


def _short_conv(x2, seq_len, w_in, k, w_out, g, b):
    m, d = x2.shape
    ts = MIX_TILE
    assert seq_len % ts == 0 and m % seq_len == 0 and SHORT_CONV_WIDTH - 1 <= SHORT_HALO <= ts
    assert d % MXU_COLS == 0
    n = m // ts
    return pl.pallas_call(
        functools.partial(_short_kernel, n=n, tiles_per_seq=seq_len // ts, ts=ts, d=d),
        grid=(n + 1,),
        in_specs=[
            _tile_in_spec(ts, d, n),
            _const_spec(w_in.shape),
            _const_spec(k.shape),
            _const_spec(w_out.shape),
            _const_spec((1, d)),
            _const_spec((1, d)),
        ],
        out_specs=_tile_out_spec(ts, d),
        out_shape=jax.ShapeDtypeStruct((m, d), F32),
        scratch_shapes=[pltpu.VMEM((SHORT_HALO + ts, d), F32), pltpu.VMEM((ts, d), BF16),
                        pltpu.VMEM((ts, d), F32)],
        compiler_params=_params(1),
        name="short_conv",
    )(x2, w_in, k, w_out, g.reshape(1, d), b.reshape(1, d))


def kernel(x, ln_g, ln_b, ffn_w_in, ffn_w_out, pool_w, pool_scale, gmlp_w_in, gmlp_v_ln_g, gmlp_v_ln_b, gmlp_ws, gmlp_bs, gmlp_w_out, conv_w_in, conv_b_in, conv_dw, conv_dw_b, conv_ln_g, conv_ln_b, conv_w_out, conv_b_out, sc_w_in, sc_conv, sc_w_out):
    bsz, s, d = x.shape
    m = bsz * s
    ffn_w_in = ffn_w_in.astype(BF16)
    ffn_w_out = ffn_w_out.astype(BF16)
    x = x.reshape(m, d)
    for i in range(DEPTH):
        mi, j = i % N_MIXERS, i // N_MIXERS
        x = _ffn(x, ffn_w_in[i, 0], ffn_w_out[i, 0], ln_g[i, 0], ln_b[i, 0])
        if mi == 0:
            x = _pool_mixer(x.reshape(bsz, s, d), pool_w[j].astype(BF16), pool_scale[j],
                            ln_g[i, 1], ln_b[i, 1]).reshape(m, d)
        elif mi == 1:
            x = _gmlp_mixer(x, gmlp_w_in[j].astype(BF16), gmlp_v_ln_g[j], gmlp_v_ln_b[j],
                            gmlp_ws[j], gmlp_bs[j], gmlp_w_out[j].astype(BF16),
                            ln_g[i, 1], ln_b[i, 1])
        elif mi == 2:
            x = _conformer_conv(x, s, conv_w_in[j].astype(BF16), conv_b_in[j],
                                conv_dw[j], conv_dw_b[j], conv_ln_g[j], conv_ln_b[j],
                                conv_w_out[j].astype(BF16), conv_b_out[j], ln_g[i, 1], ln_b[i, 1])
        else:
            x = _short_conv(x, s, sc_w_in[j].astype(BF16), sc_conv[j],
                            sc_w_out[j].astype(BF16), ln_g[i, 1], ln_b[i, 1])
        x = _ffn(x, ffn_w_in[i, 1], ffn_w_out[i, 1], ln_g[i, 2], ln_b[i, 2])
    return x.reshape(bsz, s, d)
```

```python
import functools

import jax
import jax.numpy as jnp
from jax import lax
from jax.experimental import pallas as pl
from jax.experimental.pallas import tpu as pltpu

DEPTH = 4
N_MIXERS = 4
POOL_WINDOWS = (2, 4, 8, 16)
GMLP_CHUNK = 128
CONV_WIDTH = 31
SHORT_CONV_WIDTH = 3
FFN_RES_WEIGHT = 0.5
DEEPNORM_ALPHA = (2 * DEPTH) ** 0.25
LN_EPS = 1e-5

BF16 = jnp.bfloat16
F32 = jnp.float32

SUBLANES = 8
LANES = 128
MXU_COLS = 256
PAIR_COLS = 2 * MXU_COLS
CONV_FRAMES = 8
POOL_HALO = 16
CONV_HALO = 32
SHORT_HALO = 8
VMEM_LIMIT = 56 * 1024 * 1024

FFN_TILE = 512
FFN_SUBTILES = 2
MIX_TILE = 512


def _layer_norm(r, g, b):
    mu = jnp.mean(r, axis=-1, keepdims=True)
    d = r - mu
    var = jnp.mean(d * d, axis=-1, keepdims=True)
    return d * lax.rsqrt(var + LN_EPS) * g + b


def _zero_after(v):
    u = lax.bitcast_convert_type(v, jnp.uint32)
    u = lax.shift_right_logical(lax.shift_right_logical(u, jnp.uint32(16)), jnp.uint32(16))
    return lax.bitcast_convert_type(u, F32)


def _dot(a, b):
    return jnp.dot(a, b, preferred_element_type=F32)


def _const_spec(shape):
    nd = len(shape)
    return pl.BlockSpec(shape, lambda *_: (0,) * nd, pipeline_mode=pl.Buffered(1))


def _params(n_axes):
    return pltpu.CompilerParams(
        dimension_semantics=("arbitrary",) * n_axes,
        vmem_limit_bytes=VMEM_LIMIT,
    )


def _delayed_norm(i, n, r_ref, g_ref, b_ref, o_ref, main):
    @pl.when(i == 0)
    def _():
        r_ref[...] = jnp.zeros_like(r_ref)

    @pl.when(i < n)
    def _():
        prev = _layer_norm(r_ref[...], g_ref[...], b_ref[...])
        o_ref[...] = prev
        main(_zero_after(prev))

    @pl.when(i == n)
    def _():
        o_ref[...] = _layer_norm(r_ref[...], g_ref[...], b_ref[...])


def _tile_in_spec(tm, d, n):
    return pl.BlockSpec((tm, d), lambda i: (jnp.minimum(i, n - 1), 0))


def _tile_out_spec(tm, d):
    return pl.BlockSpec((tm, d), lambda i: (jnp.maximum(i - 1, 0), 0))


def _ffn_kernel(x_ref, win_ref, wout_ref, g_ref, b_ref, o_ref, act_ref, r_ref, *, n, d_ff, sub):
    def main(anchor):
        for s, lo_r in enumerate(range(0, x_ref.shape[0], sub)):
            rows = slice(lo_r, lo_r + sub)
            x = x_ref[rows, :]
            xb = x.astype(BF16)
            for lo in range(0, d_ff, MXU_COLS):
                gate = _dot(xb, win_ref[:, lo:lo + MXU_COLS])
                up = _dot(xb, win_ref[:, d_ff + lo:d_ff + lo + MXU_COLS])
                if lo + MXU_COLS <= anchor.shape[1]:
                    up = up + anchor[rows, lo:lo + MXU_COLS]
                act_ref[s, :, lo:lo + MXU_COLS] = (jax.nn.silu(gate) * up).astype(BF16)
            y = _dot(act_ref[s], wout_ref[...])
            r_ref[rows, :] = DEEPNORM_ALPHA * x + FFN_RES_WEIGHT * y

    _delayed_norm(pl.program_id(0), n, r_ref, g_ref, b_ref, o_ref, main)


def _ffn(x2, w_in_all, w_out_all, layer, which, g, b):
    m, d = x2.shape
    d_ff = w_out_all.shape[2]
    sub = FFN_TILE
    tm = sub * FFN_SUBTILES
    assert m % tm == 0 and d_ff % MXU_COLS == 0
    n = m // tm

    def weight_spec(w):
        return pl.BlockSpec((None, None) + w.shape[2:], lambda i: (layer, which, 0, 0),
                            pipeline_mode=pl.Buffered(1))

    return pl.pallas_call(
        functools.partial(_ffn_kernel, n=n, d_ff=d_ff, sub=sub),
        grid=(n + 1,),
        in_specs=[
            _tile_in_spec(tm, d, n),
            weight_spec(w_in_all),
            weight_spec(w_out_all),
            _const_spec((1, d)),
            _const_spec((1, d)),
        ],
        out_specs=_tile_out_spec(tm, d),
        out_shape=jax.ShapeDtypeStruct((m, d), F32),
        scratch_shapes=[pltpu.VMEM((FFN_SUBTILES, sub, d_ff), BF16), pltpu.VMEM((tm, d), F32)],
        compiler_params=_params(1),
        name="ffn",
    )(x2, w_in_all, w_out_all, g.reshape(1, d), b.reshape(1, d))


def _pool_kernel(x_ref, halo_ref, w_ref, scale_ref, g_ref, b_ref, o_ref, *, ts, cg):
    j = pl.program_id(1)
    x = x_ref[0]
    halo = jnp.where(j > 0, halo_ref[0], 0.0)
    t = j * ts + lax.broadcasted_iota(jnp.int32, (ts, 1), 0)
    ys = []
    for gi, win in enumerate(POOL_WINDOWS):
        cols = slice(gi * cg, (gi + 1) * cg)
        xe = jnp.concatenate([halo[:, cols], x[:, cols]], axis=0)
        s = xe
        span = 1
        while span < win:
            s = s + pltpu.roll(s, span, axis=0)
            span *= 2
        s = s[POOL_HALO:]
        inv_cnt = 1.0 / jnp.minimum(t + 1, win).astype(F32)
        p = s * inv_cnt - x[:, cols]
        ys.append(_dot(p.astype(BF16), w_ref[gi]))
    y = jnp.concatenate(ys, axis=-1) * scale_ref[...]
    r = DEEPNORM_ALPHA * x + y
    o_ref[0] = _layer_norm(r, g_ref[...], b_ref[...])


def _pool_mixer(x, w_grp, scale, g, b):
    bsz, s, d = x.shape
    n_g, cg, _ = w_grp.shape
    ts = MIX_TILE
    assert s % ts == 0 and ts % POOL_HALO == 0 and max(POOL_WINDOWS) <= POOL_HALO
    hb = ts // POOL_HALO
    return pl.pallas_call(
        functools.partial(_pool_kernel, ts=ts, cg=cg),
        grid=(bsz, s // ts),
        in_specs=[
            pl.BlockSpec((1, ts, d), lambda bi, j: (bi, j, 0)),
            pl.BlockSpec((1, POOL_HALO, d), lambda bi, j: (bi, jnp.maximum(j * hb - 1, 0), 0)),
            _const_spec(w_grp.shape),
            _const_spec((1, d)),
            _const_spec((1, d)),
            _const_spec((1, d)),
        ],
        out_specs=pl.BlockSpec((1, ts, d), lambda bi, j: (bi, j, 0)),
        out_shape=jax.ShapeDtypeStruct(x.shape, F32),
        compiler_params=_params(2),
        name="pool_mixer",
    )(x, x, w_grp, scale.reshape(1, d), g.reshape(1, d), b.reshape(1, d))


def _gmlp_kernel(x_ref, win_ref, vg_ref, vb_ref, ws_ref, bst_ref, wout_ref, g_ref, b_ref,
                 o_ref, v_ref, gated_ref, r_ref, *, n, tm, e, n_h):
    l = GMLP_CHUNK
    dh = e // n_h

    def main(anchor):
        x = x_ref[...]
        xb = x.astype(BF16)
        row_sum = jnp.zeros((tm, 1), F32)
        for c in range(0, e, PAIR_COLS):
            vc = _dot(xb, win_ref[:, e + c:e + c + PAIR_COLS])
            if c + PAIR_COLS <= anchor.shape[1]:
                vc = vc + anchor[:, c:c + PAIR_COLS]
            vc = jax.nn.gelu(vc)
            v_ref[:, c:c + PAIR_COLS] = vc
            row_sum = row_sum + jnp.sum(vc, axis=-1, keepdims=True)
        mu = row_sum * (1.0 / e)
        sq_sum = jnp.zeros((tm, 1), F32)
        for c in range(0, e, PAIR_COLS):
            dv = v_ref[:, c:c + PAIR_COLS] - mu
            sq_sum = sq_sum + jnp.sum(dv * dv, axis=-1, keepdims=True)
        rstd = lax.rsqrt(sq_sum * (1.0 / e) + LN_EPS)

        row = lax.broadcasted_iota(jnp.int32, (l, l), 0)
        col = lax.broadcasted_iota(jnp.int32, (l, l), 1)
        causal = col <= row
        for c in range(0, e, PAIR_COLS):
            pair = slice(c, c + PAIR_COLS)
            vn = ((v_ref[:, pair] - mu) * rstd * vg_ref[:, pair] + vb_ref[:, pair]).astype(BF16)
            u = jax.nn.gelu(_dot(xb, win_ref[:, pair]))
            for q in range(0, PAIR_COLS, dh):
                h = (c + q) // dh
                w_h = jnp.where(causal, ws_ref[h], 0.0).astype(BF16)
                bias = bst_ref[:, h:h + 1]
                for rc in range(0, tm, l):
                    sv = _dot(w_h, vn[rc:rc + l, q:q + dh]) + bias
                    gated_ref[rc:rc + l, c + q:c + q + dh] = (u[rc:rc + l, q:q + dh] * sv).astype(BF16)
        y = _dot(gated_ref[...], wout_ref[...])
        r_ref[...] = DEEPNORM_ALPHA * x + y

    _delayed_norm(pl.program_id(0), n, r_ref, g_ref, b_ref, o_ref, main)


def _gmlp_mixer(x2, w_in, v_g, v_b, ws, bs, w_out, g, b):
    m, d = x2.shape
    e = w_out.shape[0]
    n_h, l, _ = ws.shape
    tm = MIX_TILE
    assert m % tm == 0 and tm % l == 0 and l == GMLP_CHUNK
    assert e % PAIR_COLS == 0 and PAIR_COLS % (e // n_h) == 0 and (e // n_h) % LANES == 0
    n = m // tm
    return pl.pallas_call(
        functools.partial(_gmlp_kernel, n=n, tm=tm, e=e, n_h=n_h),
        grid=(n + 1,),
        in_specs=[
            _tile_in_spec(tm, d, n),
            _const_spec(w_in.shape),
            _const_spec((1, e)),
            _const_spec((1, e)),
            _const_spec(ws.shape),
            _const_spec((l, n_h)),
            _const_spec(w_out.shape),
            _const_spec((1, d)),
            _const_spec((1, d)),
        ],
        out_specs=_tile_out_spec(tm, d),
        out_shape=jax.ShapeDtypeStruct((m, d), F32),
        scratch_shapes=[pltpu.VMEM((tm, e), F32), pltpu.VMEM((tm, e), BF16),
                        pltpu.VMEM((tm, d), F32)],
        compiler_params=_params(1),
        name="gmlp_mixer",
    )(x2, w_in, v_g.reshape(1, e), v_b.reshape(1, e), ws, bs.T, w_out,
      g.reshape(1, d), b.reshape(1, d))


def _conv_kernel(x_ref, win_ref, bin_ref, dw_ref, dwb_ref, cg_ref, cb_ref, wout_ref, bout_ref,
                 g_ref, b_ref, o_ref, z_ref, cz_ref, r_ref, *, n, tiles_per_seq, ts, d):
    i = pl.program_id(0)
    nl = d // LANES

    @pl.when(i % tiles_per_seq == 0)
    def _():
        z_ref[:CONV_HALO * nl, :] = jnp.zeros((CONV_HALO * nl, LANES), F32)

    def main(anchor):
        x = x_ref[...]
        xb = x.astype(BF16)
        for c in range(0, d, MXU_COLS):
            a = _dot(xb, win_ref[:, c:c + MXU_COLS]) + bin_ref[:, c:c + MXU_COLS]
            a = a + anchor[:, c:c + MXU_COLS]
            gt = _dot(xb, win_ref[:, d + c:d + c + MXU_COLS]) + bin_ref[:, d + c:d + c + MXU_COLS]
            h = a * jax.nn.sigmoid(gt)
            for q in range(MXU_COLS // LANES):
                lt = c // LANES + q
                z_ref[pl.ds(CONV_HALO * nl + lt, ts, stride=nl), :] = h[:, q * LANES:(q + 1) * LANES]

        base = CONV_HALO - (CONV_WIDTH - 1)
        bias = dwb_ref[...]
        for t0 in range(0, ts, CONV_FRAMES):
            frames = {}
            accs = [bias] * CONV_FRAMES
            for k in range(CONV_WIDTH):
                w = dw_ref[k * nl:(k + 1) * nl, :]
                for f in range(CONV_FRAMES):
                    e = t0 + base + f + k
                    if e not in frames:
                        frames[e] = z_ref[e * nl:(e + 1) * nl, :]
                    accs[f] = accs[f] + frames[e] * w
            for f in range(CONV_FRAMES):
                cz_ref[(t0 + f) * nl:(t0 + f + 1) * nl, :] = accs[f]
        z_ref[:CONV_HALO * nl, :] = z_ref[ts * nl:(ts + CONV_HALO) * nl, :]

        acc = jnp.concatenate([cz_ref[pl.ds(lt, ts, stride=nl), :] for lt in range(nl)], axis=-1)
        hn = jax.nn.silu(_layer_norm(acc, cg_ref[...], cb_ref[...]))
        y = _dot(hn.astype(BF16), wout_ref[...]) + bout_ref[...]
        r_ref[...] = DEEPNORM_ALPHA * x + y

    _delayed_norm(i, n, r_ref, g_ref, b_ref, o_ref, main)


def _conformer_conv(x2, seq_len, w_in, b_in, dw, dw_b, c_g, c_b, w_out, b_out, g, b):
    m, d = x2.shape
    ts = MIX_TILE
    assert seq_len % ts == 0 and m % seq_len == 0 and CONV_WIDTH - 1 <= CONV_HALO <= ts
    assert d % MXU_COLS == 0 and ts % CONV_FRAMES == 0
    nl = d // LANES
    n_taps = dw.shape[0]
    n = m // ts
    return pl.pallas_call(
        functools.partial(_conv_kernel, n=n, tiles_per_seq=seq_len // ts, ts=ts, d=d),
        grid=(n + 1,),
        in_specs=[
            _tile_in_spec(ts, d, n),
            _const_spec(w_in.shape),
            _const_spec((1, 2 * d)),
            _const_spec((n_taps * nl, LANES)),
            _const_spec((nl, LANES)),
            _const_spec((1, d)),
            _const_spec((1, d)),
            _const_spec(w_out.shape),
            _const_spec((1, d)),
            _const_spec((1, d)),
            _const_spec((1, d)),
        ],
        out_specs=_tile_out_spec(ts, d),
        out_shape=jax.ShapeDtypeStruct((m, d), F32),
        scratch_shapes=[pltpu.VMEM(((CONV_HALO + ts) * nl, LANES), F32),
                        pltpu.VMEM((ts * nl, LANES), F32),
                        pltpu.VMEM((ts, d), F32)],
        compiler_params=_params(1),
        name="conformer_conv",
    )(x2, w_in, b_in.reshape(1, 2 * d), dw.reshape(n_taps * nl, LANES), dw_b.reshape(nl, LANES),
      c_g.reshape(1, d), c_b.reshape(1, d), w_out, b_out.reshape(1, d), g.reshape(1, d),
      b.reshape(1, d))


def _short_kernel(x_ref, win_ref, k_ref, wout_ref, g_ref, b_ref, o_ref, ce_ref, gated_ref, r_ref,
                  *, n, tiles_per_seq, ts, d):
    i = pl.program_id(0)

    @pl.when(i % tiles_per_seq == 0)
    def _():
        ce_ref[:SHORT_HALO, :] = jnp.zeros((SHORT_HALO, d), F32)

    def main(anchor):
        x = x_ref[...]
        xb = x.astype(BF16)
        for c in range(0, d, MXU_COLS):
            c_gate = _dot(xb, win_ref[:, d + c:d + c + MXU_COLS]) + anchor[:, c:c + MXU_COLS]
            hh = _dot(xb, win_ref[:, 2 * d + c:2 * d + c + MXU_COLS])
            ce_ref[SHORT_HALO:, c:c + MXU_COLS] = c_gate * hh
        base = SHORT_HALO - (SHORT_CONV_WIDTH - 1)
        for c in range(0, d, PAIR_COLS):
            cols = slice(c, c + PAIR_COLS)
            b_gate = _dot(xb, win_ref[:, cols])
            acc = ce_ref[base:base + ts, cols] * k_ref[0:1, cols]
            for k in range(1, SHORT_CONV_WIDTH):
                acc = acc + ce_ref[base + k:base + k + ts, cols] * k_ref[k:k + 1, cols]
            gated_ref[:, cols] = (b_gate * acc).astype(BF16)
        ce_ref[:SHORT_HALO, :] = ce_ref[ts:ts + SHORT_HALO, :]

        y = _dot(gated_ref[...], wout_ref[...])
        r_ref[...] = DEEPNORM_ALPHA * x + y

    _delayed_norm(i, n, r_ref, g_ref, b_ref, o_ref, main)


def _short_conv(x2, seq_len, w_in, k, w_out, g, b):
    m, d = x2.shape
    ts = MIX_TILE
    assert seq_len % ts == 0 and m % seq_len == 0 and SHORT_CONV_WIDTH - 1 <= SHORT_HALO <= ts
    assert d % MXU_COLS == 0
    n = m // ts
    return pl.pallas_call(
        functools.partial(_short_kernel, n=n, tiles_per_seq=seq_len // ts, ts=ts, d=d),
        grid=(n + 1,),
        in_specs=[
            _tile_in_spec(ts, d, n),
            _const_spec(w_in.shape),
            _const_spec(k.shape),
            _const_spec(w_out.shape),
            _const_spec((1, d)),
            _const_spec((1, d)),
        ],
        out_specs=_tile_out_spec(ts, d),
        out_shape=jax.ShapeDtypeStruct((m, d), F32),
        scratch_shapes=[pltpu.VMEM((SHORT_HALO + ts, d), F32), pltpu.VMEM((ts, d), BF16),
                        pltpu.VMEM((ts, d), F32)],
        compiler_params=_params(1),
        name="short_conv",
    )(x2, w_in, k, w_out, g.reshape(1, d), b.reshape(1, d))


def kernel(x, ln_g, ln_b, ffn_w_in, ffn_w_out, pool_w, pool_scale, gmlp_w_in, gmlp_v_ln_g, gmlp_v_ln_b, gmlp_ws, gmlp_bs, gmlp_w_out, conv_w_in, conv_b_in, conv_dw, conv_dw_b, conv_ln_g, conv_ln_b, conv_w_out, conv_b_out, sc_w_in, sc_conv, sc_w_out):
    bsz, s, d = x.shape
    m = bsz * s
    ffn_w_in = ffn_w_in.astype(BF16)
    ffn_w_out = ffn_w_out.astype(BF16)
    x = x.reshape(m, d)
    for i in range(DEPTH):
        mi, j = i % N_MIXERS, i // N_MIXERS
        x = _ffn(x, ffn_w_in, ffn_w_out, i, 0, ln_g[i, 0], ln_b[i, 0])
        if mi == 0:
            x = _pool_mixer(x.reshape(bsz, s, d), pool_w[j].astype(BF16), pool_scale[j],
                            ln_g[i, 1], ln_b[i, 1]).reshape(m, d)
        elif mi == 1:
            x = _gmlp_mixer(x, gmlp_w_in[j].astype(BF16), gmlp_v_ln_g[j], gmlp_v_ln_b[j],
                            gmlp_ws[j], gmlp_bs[j], gmlp_w_out[j].astype(BF16),
                            ln_g[i, 1], ln_b[i, 1])
        elif mi == 2:
            x = _conformer_conv(x, s, conv_w_in[j].astype(BF16), conv_b_in[j],
                                conv_dw[j], conv_dw_b[j], conv_ln_g[j], conv_ln_b[j],
                                conv_w_out[j].astype(BF16), conv_b_out[j], ln_g[i, 1], ln_b[i, 1])
        else:
            x = _short_conv(x, s, sc_w_in[j].astype(BF16), sc_conv[j],
                            sc_w_out[j].astype(BF16), ln_g[i, 1], ln_b[i, 1])
        x = _ffn(x, ffn_w_in, ffn_w_out, i, 1, ln_g[i, 2], ln_b[i, 2])
    return x.reshape(bsz, s, d)
```
